```python
import math
import jax, jax.numpy as jnp
from jax import lax
import numpy as np

D_MODEL = 4096
BATCH = 1
SEQ = 8192
DEPTH = 2

HEAD_DIM = 128
A_GROUPS = 16
A_WIDTH = A_GROUPS * HEAD_DIM
CHUNK = 128
B_HEADS = 16
B_WIDTH = B_HEADS * HEAD_DIM
DILATED_CONFIGS = ((128, 1), (512, 4), (2048, 16))
ATTN_BLOCK = 128
ROPE_THETA = 500000.0
ROT_DIM = HEAD_DIM // 4
D_FF = ((8 * D_MODEL // 3 + 255) // 256) * 256
PL_DIM = 256
IN_WIDTH = 2 * A_WIDTH + 3 * B_WIDTH + 2 * D_MODEL
EPS = 1e-6
NEG_INF = -1e30

kernel_name = "hybrid_gmlp_dilated_attn_block"


def rms_norm(x, g):
    xf = x.astype(jnp.float32)
    y = xf * lax.rsqrt(jnp.mean(xf * xf, axis=-1, keepdims=True) + EPS)
    return (y * g.astype(jnp.float32)).astype(x.dtype)


def layer_norm(x, g, b):
    xf = x.astype(jnp.float32)
    mu = jnp.mean(xf, axis=-1, keepdims=True)
    var = jnp.mean(jnp.square(xf - mu), axis=-1, keepdims=True)
    y = (xf - mu) * lax.rsqrt(var + EPS)
    return (y * g.astype(jnp.float32) + b.astype(jnp.float32)).astype(x.dtype)


def partial_rotary(x, positions):
    half = ROT_DIM // 2
    inv_freq = jnp.power(jnp.float32(ROPE_THETA), -jnp.arange(half, dtype=jnp.float32) / half)
    ang = positions.astype(jnp.float32)[..., None] * inv_freq
    cos = jnp.cos(ang)[:, :, None, :]
    sin = jnp.sin(ang)[:, :, None, :]
    xf = x.astype(jnp.float32)
    x1 = xf[..., :half]
    x2 = xf[..., half:ROT_DIM]
    out = jnp.concatenate([x1 * cos - x2 * sin, x2 * cos + x1 * sin, xf[..., ROT_DIM:]], axis=-1)
    return out.astype(x.dtype)


def chunked_spatial_gating(u, v, ln_g, ln_b, w_s, b_s):
    B, S, _ = v.shape
    vn = layer_norm(v, ln_g, ln_b)
    vc = vn.reshape(B, S // CHUNK, CHUNK, A_GROUPS, HEAD_DIM)
    causal = jnp.tril(jnp.ones((CHUNK, CHUNK), dtype=bool))
    w = jnp.where(causal[None], w_s, jnp.zeros((), w_s.dtype))
    s = jnp.einsum('gts,bnsgc->bntgc', w, vc) + b_s.T[None, None, :, :, None]
    return u * s.reshape(B, S, A_WIDTH)


def dilated_branch(q, k, v, dilation, n_steps):
    B, S, H, Dh = q.shape
    L = S // dilation
    Lp = -(-L // ATTN_BLOCK) * ATTN_BLOCK
    nb = Lp // ATTN_BLOCK

    def to_blocks(t):
        t = t.reshape(B, L, dilation, H, Dh).transpose(0, 2, 3, 1, 4)
        t = jnp.pad(t, ((0, 0), (0, 0), (0, 0), (0, Lp - L), (0, 0)))
        return t.reshape(B, dilation, H, nb, ATTN_BLOCK, Dh)

    def with_prev(t):
        prev = jnp.pad(t, ((0, 0), (0, 0), (0, 0), (1, 0), (0, 0), (0, 0)))[:, :, :, :-1]
        return jnp.concatenate([prev, t], axis=4)

    qb = to_blocks(q)
    kb = with_prev(to_blocks(k))
    vb = with_prev(to_blocks(v))
    s = jnp.einsum('bdhnqe,bdhnke->bdhnqk', qb, kb, preferred_element_type=jnp.float32)
    qi = jnp.arange(ATTN_BLOCK)[:, None]
    kj = jnp.arange(2 * ATTN_BLOCK)[None, :]
    delta = qi + ATTN_BLOCK - kj
    band = (delta >= 0) & (delta <= n_steps)
    has_prev = (jnp.arange(nb) > 0)[:, None, None]
    mask = band[None] & (has_prev | (kj >= ATTN_BLOCK)[None])
    s = jnp.where(mask, s, NEG_INF)
    m = jnp.max(s, axis=-1, keepdims=True)
    e = jnp.exp(s - m)
    l = jnp.sum(e, axis=-1, keepdims=True)
    o = jnp.einsum('bdhnqk,bdhnke->bdhnqe', e, vb.astype(jnp.float32)) / l

    def from_blocks(t):
        c = t.shape[-1]
        t = t.reshape(B, dilation, H, Lp, c)[:, :, :, :L]
        return t.transpose(0, 3, 1, 2, 4).reshape(B, S, H, c)

    return from_blocks(o), from_blocks(m)[..., 0], from_blocks(l)[..., 0]


def dilated_attention(q, k, v):
    outs, maxes, dens = [], [], []
    for window, dilation in DILATED_CONFIGS:
        o, m, l = dilated_branch(q, k, v, dilation, window // dilation)
        outs.append(o)
        maxes.append(m)
        dens.append(l)
    o = jnp.stack(outs)
    m = jnp.stack(maxes)
    l = jnp.stack(dens)
    wts = jnp.exp(m - jnp.max(m, axis=0, keepdims=True)) * l
    out = jnp.sum(wts[..., None] * o, axis=0) / jnp.sum(wts, axis=0)[..., None]
    return out.astype(q.dtype)


def setup_inputs(seed: int = 0) -> dict:
    key = jax.random.key(seed)
    ks = jax.random.split(key, 24)
    f32 = jnp.float32

    def nrm(k, shape, fan_in):
        return jax.random.normal(k, shape, f32) * (fan_in ** -0.5)

    def gain(k, shape):
        return 1.0 + 0.05 * jax.random.normal(k, shape, f32)

    return {
        "x": jax.random.normal(ks[0], (BATCH, SEQ, D_MODEL), f32),
        "p": jax.random.normal(ks[1], (DEPTH, BATCH, SEQ, PL_DIM), f32),
        "positions": jnp.broadcast_to(jnp.arange(SEQ, dtype=jnp.int32)[None, :], (BATCH, SEQ)),
        "norm_mix": gain(ks[2], (DEPTH, D_MODEL)),
        "w_in": nrm(ks[3], (DEPTH, D_MODEL, IN_WIDTH), D_MODEL),
        "v_ln_g": gain(ks[4], (DEPTH, A_WIDTH)),
        "v_ln_b": 0.01 * jax.random.normal(ks[5], (DEPTH, A_WIDTH), f32),
        "w_spatial": nrm(ks[6], (DEPTH, A_GROUPS, CHUNK, CHUNK), CHUNK),
        "b_spatial": 1.0 + 0.05 * jax.random.normal(ks[7], (DEPTH, A_GROUPS, CHUNK), f32),
        "q_norm": gain(ks[8], (DEPTH, HEAD_DIM)),
        "k_norm": gain(ks[9], (DEPTH, HEAD_DIM)),
        "w_br_a": nrm(ks[10], (DEPTH, A_WIDTH, D_MODEL), A_WIDTH),
        "w_br_b": nrm(ks[11], (DEPTH, B_WIDTH, D_MODEL), B_WIDTH),
        "w_out": nrm(ks[12], (DEPTH, D_MODEL, D_MODEL), D_MODEL),
        "norm_ffn": gain(ks[13], (DEPTH, D_MODEL)),
        "w_ffn_gate": nrm(ks[14], (DEPTH, D_MODEL, D_FF), D_MODEL),
        "w_ffn_up": nrm(ks[15], (DEPTH, D_MODEL, D_FF), D_MODEL),
        "w_ffn_down": nrm(ks[16], (DEPTH, D_FF, D_MODEL), D_FF),
        "norm_pl": gain(ks[17], (DEPTH, D_MODEL)),
        "w_pl_gate": nrm(ks[18], (DEPTH, D_MODEL, D_MODEL), D_MODEL),
        "w_pl_proj": nrm(ks[19], (DEPTH, PL_DIM, D_MODEL), PL_DIM),
    }


def reference(x, p, positions, norm_mix, w_in, v_ln_g, v_ln_b, w_spatial, b_spatial,
              q_norm, k_norm, w_br_a, w_br_b, w_out, norm_ffn, w_ffn_gate, w_ffn_up,
              w_ffn_down, norm_pl, w_pl_gate, w_pl_proj):
    B, S, _ = x.shape
    splits = [A_WIDTH, 2 * A_WIDTH, 2 * A_WIDTH + B_WIDTH, 2 * A_WIDTH + 2 * B_WIDTH,
              2 * A_WIDTH + 3 * B_WIDTH, 2 * A_WIDTH + 3 * B_WIDTH + D_MODEL]
    h = x
    for i in range(DEPTH):
        xn = rms_norm(h, norm_mix[i])
        z = xn @ w_in[i]
        u, va, q, k, v, g_a, g_b = jnp.split(z, splits, axis=-1)
        a_out = chunked_spatial_gating(u, va, v_ln_g[i], v_ln_b[i], w_spatial[i], b_spatial[i])
        q = q.reshape(B, S, B_HEADS, HEAD_DIM)
        k = k.reshape(B, S, B_HEADS, HEAD_DIM)
        v = v.reshape(B, S, B_HEADS, HEAD_DIM)
        q = partial_rotary(rms_norm(q, q_norm[i]), positions) * (HEAD_DIM ** -0.5)
        k = partial_rotary(rms_norm(k, k_norm[i]), positions)
        b_out = dilated_attention(q, k, v).reshape(B, S, B_WIDTH)
        merged = jax.nn.sigmoid(g_a) * (a_out @ w_br_a[i]) + jax.nn.sigmoid(g_b) * (b_out @ w_br_b[i])
        h = h + merged @ w_out[i]
        hn = rms_norm(h, norm_ffn[i])
        h = h + (jax.nn.silu(hn @ w_ffn_gate[i]) * (hn @ w_ffn_up[i])) @ w_ffn_down[i]
        hp = rms_norm(h, norm_pl[i])
        h = h + jax.nn.sigmoid(hp @ w_pl_gate[i]) * (p[i] @ w_pl_proj[i])
    return h
```

```python
import functools

import jax
import jax.numpy as jnp
from jax import lax
from jax.experimental import pallas as pl
from jax.experimental.pallas import tpu as pltpu

F32 = jnp.float32
BF16 = jnp.bfloat16

HEAD_DIM = 128
CHUNK = 128
ATTN_BLOCK = 128
DILATIONS = (1, 4, 16)
N_STEPS = 128
ROPE_THETA = 500000.0
ROT_DIM = HEAD_DIM // 4
ROT_HALF = ROT_DIM // 2
EPS = 1e-6
NEG_INF = -1e30

VMEM_LIMIT_BYTES = 56 * 1024 * 1024


def _params(n_grid_dims):
    return pltpu.CompilerParams(
        dimension_semantics=("arbitrary",) * n_grid_dims,
        vmem_limit_bytes=VMEM_LIMIT_BYTES,
    )


def _sigmoid(x):
    return 1.0 / (1.0 + jnp.exp(-x))


def _rmsnorm_kernel(x_ref, g_ref, o_ref):
    x = x_ref[...]
    ms = jnp.mean(x * x, axis=-1, keepdims=True)
    o_ref[...] = (x * lax.rsqrt(ms + EPS) * g_ref[...]).astype(o_ref.dtype)


def _rmsnorm(x, g, *, tm=256):
    m, d = x.shape
    return pl.pallas_call(
        _rmsnorm_kernel,
        out_shape=jax.ShapeDtypeStruct((m, d), BF16),
        grid=(m // tm,),
        in_specs=[pl.BlockSpec((tm, d), lambda i: (i, 0)),
                  pl.BlockSpec((1, d), lambda i: (0, 0))],
        out_specs=pl.BlockSpec((tm, d), lambda i: (i, 0)),
        compiler_params=_params(1),
        name="rmsnorm",
    )(x, g.reshape(1, d))


def _dot(x, w_ref):
    return jnp.dot(x, w_ref[...].astype(BF16), preferred_element_type=F32)


def _mm_plain_kernel(x_ref, w_ref, o_ref):
    o_ref[...] = _dot(x_ref[...], w_ref).astype(o_ref.dtype)


def _mm_gate_kernel(x_ref, w_ref, gate_ref, o_ref):
    o_ref[...] = (_sigmoid(gate_ref[...].astype(F32)) * _dot(x_ref[...], w_ref)).astype(o_ref.dtype)


def _mm_gate_add_kernel(x_ref, w_ref, gate_ref, add_ref, o_ref):
    o_ref[...] = (add_ref[...] + _sigmoid(gate_ref[...].astype(F32)) * _dot(x_ref[...], w_ref)).astype(o_ref.dtype)


def _mm_add_kernel(x_ref, w_ref, add_ref, o_ref):
    o_ref[...] = (add_ref[...] + _dot(x_ref[...], w_ref)).astype(o_ref.dtype)


def _mm_plgate_kernel(x_ref, w_ref, p_ref, wp_ref, add_ref, o_ref):
    emb = _dot(p_ref[...].astype(BF16), wp_ref)
    o_ref[...] = (add_ref[...] + _sigmoid(_dot(x_ref[...], w_ref)) * emb).astype(o_ref.dtype)


def _x_spec(tm, k, kb):
    return pl.BlockSpec((tm, k), lambda i, j: (i, kb), pipeline_mode=pl.Buffered(1))


def _matmul(kernel, x, w, extras, extra_specs, *, out_dtype, tm, tn, kb=0, kblk=None, wcol0=0, name):
    m = x.shape[0]
    kblk = x.shape[1] if kblk is None else kblk
    n = w.shape[1]
    return pl.pallas_call(
        kernel,
        out_shape=jax.ShapeDtypeStruct((m, n), out_dtype),
        grid=(m // tm, n // tn),
        in_specs=[_x_spec(tm, kblk, kb),
                  pl.BlockSpec((kblk, tn), lambda i, j: (kb, j))] + extra_specs,
        out_specs=pl.BlockSpec((tm, tn), lambda i, j: (i, j)),
        compiler_params=_params(2),
        name=name,
    )(x, w, *extras)


def _tile_spec(tm, tn, col0=0):
    return pl.BlockSpec((tm, tn), lambda i, j: (i, j + col0))


def _ffn_up_kernel(x_ref, wg_ref, wu_ref, o_ref):
    x = x_ref[...]
    g = _dot(x, wg_ref)
    u = _dot(x, wu_ref)
    o_ref[...] = (g * _sigmoid(g) * u).astype(o_ref.dtype)


def _ffn_up(x, wg, wu, *, tm, tn):
    m, k = x.shape
    n = wg.shape[1]
    return pl.pallas_call(
        _ffn_up_kernel,
        out_shape=jax.ShapeDtypeStruct((m, n), BF16),
        grid=(m // tm, n // tn),
        in_specs=[_x_spec(tm, k, 0),
                  pl.BlockSpec((k, tn), lambda i, j: (0, j)),
                  pl.BlockSpec((k, tn), lambda i, j: (0, j))],
        out_specs=pl.BlockSpec((tm, tn), lambda i, j: (i, j)),
        compiler_params=_params(2),
        name="ffn_up",
    )(x, wg, wu)


def _rope_kernel(pos_ref, invf_ref, c_ref, sa_ref, sb_ref):
    ang = pos_ref[...].astype(F32) * invf_ref[...]
    lane = lax.broadcasted_iota(jnp.int32, ang.shape, 1)
    c = jnp.cos(ang)
    s = jnp.sin(ang)
    c_ref[...] = jnp.where(lane < ROT_DIM, c, 1.0)
    sa_ref[...] = jnp.where(lane < ROT_HALF, -s, 0.0)
    sb_ref[...] = jnp.where((lane >= ROT_HALF) & (lane < ROT_DIM), s, 0.0)


def _rope_tables(positions, *, tm=512):
    s = positions.shape[0]
    inv_freq = jnp.power(jnp.float32(ROPE_THETA), -jnp.arange(ROT_HALF, dtype=F32) / ROT_HALF)
    invf_row = jnp.tile(inv_freq, HEAD_DIM // ROT_HALF).reshape(1, HEAD_DIM)
    tab = jax.ShapeDtypeStruct((s, HEAD_DIM), F32)
    return pl.pallas_call(
        _rope_kernel,
        out_shape=(tab, tab, tab),
        grid=(s // tm,),
        in_specs=[pl.BlockSpec((tm, 1), lambda i: (i, 0)),
                  pl.BlockSpec((1, HEAD_DIM), lambda i: (0, 0))],
        out_specs=(pl.BlockSpec((tm, HEAD_DIM), lambda i: (i, 0)),) * 3,
        compiler_params=_params(1),
        name="rope_tables",
    )(positions.reshape(s, 1), invf_row)


def _qk_prep_kernel(q_ref, k_ref, qg_ref, kg_ref, c_ref, sa_ref, sb_ref, qo_ref, ko_ref, *, n_heads):
    for h in range(n_heads):
        cs = slice(h * HEAD_DIM, (h + 1) * HEAD_DIM)
        for src, g_ref, dst, scale in ((q_ref, qg_ref, qo_ref, HEAD_DIM ** -0.5), (k_ref, kg_ref, ko_ref, None)):
            x = src[:, cs].astype(F32)
            ms = jnp.mean(x * x, axis=-1, keepdims=True)
            y = x * lax.rsqrt(ms + EPS) * g_ref[...]
            y = (y * c_ref[...]
                 + pltpu.roll(y, HEAD_DIM - ROT_HALF, 1) * sa_ref[...]
                 + pltpu.roll(y, ROT_HALF, 1) * sb_ref[...])
            if scale is not None:
                y = y * scale
            dst[:, cs] = y.astype(dst.dtype)


def _qk_prep(z, q_gain, k_gain, tables, *, width, q_col, k_col, tm=256):
    s = z.shape[0]
    n_heads = width // HEAD_DIM
    out = jax.ShapeDtypeStruct((s, width), BF16)
    gain_spec = pl.BlockSpec((1, HEAD_DIM), lambda i: (0, 0))
    tab_spec = pl.BlockSpec((tm, HEAD_DIM), lambda i: (i, 0))
    return pl.pallas_call(
        functools.partial(_qk_prep_kernel, n_heads=n_heads),
        out_shape=(out, out),
        grid=(s // tm,),
        in_specs=[pl.BlockSpec((tm, width), lambda i: (i, q_col // width)),
                  pl.BlockSpec((tm, width), lambda i: (i, k_col // width)),
                  gain_spec, gain_spec, tab_spec, tab_spec, tab_spec],
        out_specs=(pl.BlockSpec((tm, width), lambda i: (i, 0)),) * 2,
        compiler_params=_params(1),
        name="qk_prep",
    )(z, z, q_gain.reshape(1, HEAD_DIM), k_gain.reshape(1, HEAD_DIM), *tables)


def _gmlp_kernel(u_ref, v_ref, lng_ref, lnb_ref, ws_ref, bs_ref, o_ref, *, n_groups, n_chunks):
    v = v_ref[...].astype(F32)
    mu = jnp.mean(v, axis=-1, keepdims=True)
    vc = v - mu
    var = jnp.mean(vc * vc, axis=-1, keepdims=True)
    vn = (vc * lax.rsqrt(var + EPS) * lng_ref[...] + lnb_ref[...]).astype(BF16)
    row = lax.broadcasted_iota(jnp.int32, (CHUNK, CHUNK), 0)
    col = lax.broadcasted_iota(jnp.int32, (CHUNK, CHUNK), 1)
    for g in range(n_groups):
        cs = slice(g * HEAD_DIM, (g + 1) * HEAD_DIM)
        w = jnp.where(row >= col, ws_ref[g], 0.0).astype(BF16)
        for c in range(n_chunks):
            rs = slice(c * CHUNK, (c + 1) * CHUNK)
            s = jnp.dot(w, vn[rs, cs], preferred_element_type=F32) + bs_ref[:, cs]
            o_ref[rs, cs] = (u_ref[rs, cs].astype(F32) * s).astype(o_ref.dtype)


def _gmlp(z, ln_g, ln_b, w_s, b_s, *, width, n_chunks=2):
    s = z.shape[0]
    n_groups = width // HEAD_DIM
    tm = n_chunks * CHUNK
    bias = jnp.repeat(b_s.T, HEAD_DIM, axis=1)
    row_spec = pl.BlockSpec((1, width), lambda i: (0, 0))
    return pl.pallas_call(
        functools.partial(_gmlp_kernel, n_groups=n_groups, n_chunks=n_chunks),
        out_shape=jax.ShapeDtypeStruct((s, width), BF16),
        grid=(s // tm,),
        in_specs=[pl.BlockSpec((tm, width), lambda i: (i, 0)),
                  pl.BlockSpec((tm, width), lambda i: (i, 1)),
                  row_spec, row_spec,
                  pl.BlockSpec((n_groups, CHUNK, CHUNK), lambda i: (0, 0, 0)),
                  pl.BlockSpec((CHUNK, width), lambda i: (0, 0))],
        out_specs=pl.BlockSpec((tm, width), lambda i: (i, 0)),
        compiler_params=_params(1),
        name="gmlp",
    )(z, z, ln_g.reshape(1, width), ln_b.reshape(1, width), w_s, bias)


def _dot_t(a, b):
    return lax.dot_general(a, b, (((1,), (1,)), ((), ())), preferred_element_type=F32)


def _attn_kernel(q_ref, kc_ref, kp_ref, vc_ref, vp_ref, o_ref, st_ref, *, n_heads, n_sub):
    first = pl.program_id(1) == 0
    qi = lax.broadcasted_iota(jnp.int32, (ATTN_BLOCK, 2 * ATTN_BLOCK), 0)
    kj = lax.broadcasted_iota(jnp.int32, (ATTN_BLOCK, 2 * ATTN_BLOCK), 1)
    band = (kj >= qi) & (kj <= qi + N_STEPS)
    band_first = band & (kj >= jnp.where(first, ATTN_BLOCK, 0))
    lane = lax.broadcasted_iota(jnp.int32, (ATTN_BLOCK, HEAD_DIM), 1)

    def softmax_parts(s, mask):
        s = jnp.where(mask, s, NEG_INF)
        m = jnp.max(s, axis=1, keepdims=True)
        e = jnp.exp(s - m)
        return m, jnp.sum(e, axis=1, keepdims=True), e.astype(BF16)

    def put_stats(st, h, m, l):
        st = jnp.where(lane == 2 * h, m, st)
        return jnp.where(lane == 2 * h + 1, l, st)

    st = jnp.zeros((ATTN_BLOCK, HEAD_DIM), F32)
    for h in range(n_heads):
        cs = slice(h * HEAD_DIM, (h + 1) * HEAD_DIM)
        q = q_ref[0:ATTN_BLOCK, cs]
        s = jnp.concatenate([_dot_t(q, kp_ref[:, cs]), _dot_t(q, kc_ref[0:ATTN_BLOCK, cs])], axis=1)
        m, l, e = softmax_parts(s, band_first)
        o_ref[0:ATTN_BLOCK, cs] = (
            jnp.dot(e[:, :ATTN_BLOCK], vp_ref[:, cs], preferred_element_type=F32)
            + jnp.dot(e[:, ATTN_BLOCK:], vc_ref[0:ATTN_BLOCK, cs], preferred_element_type=F32))
        st = put_stats(st, h, m, l)
    st_ref[0:ATTN_BLOCK, :] = st

    def sub_block(j, carry):
        q0 = pl.multiple_of(j * ATTN_BLOCK, ATTN_BLOCK)
        k0 = pl.multiple_of((j - 1) * ATTN_BLOCK, ATTN_BLOCK)
        st = jnp.zeros((ATTN_BLOCK, HEAD_DIM), F32)
        for h in range(n_heads):
            cs = slice(h * HEAD_DIM, (h + 1) * HEAD_DIM)
            s = _dot_t(q_ref[pl.ds(q0, ATTN_BLOCK), cs], kc_ref[pl.ds(k0, 2 * ATTN_BLOCK), cs])
            m, l, e = softmax_parts(s, band)
            o_ref[pl.ds(q0, ATTN_BLOCK), cs] = jnp.dot(
                e, vc_ref[pl.ds(k0, 2 * ATTN_BLOCK), cs], preferred_element_type=F32)
            st = put_stats(st, h, m, l)
        st_ref[pl.ds(q0, ATTN_BLOCK), :] = st
        return carry

    if n_sub > 1:
        lax.fori_loop(1, n_sub, sub_block, 0)


def _dilated_attention_parts(qn, kn, z, dilation, *, width, v_col, n_sub=4):
    s = qn.shape[0]
    n_heads = width // HEAD_DIM
    sub_len = s // dilation
    tq = n_sub * ATTN_BLOCK
    zw = z.shape[1]
    q2 = qn.reshape(sub_len, dilation * width)
    k2 = kn.reshape(sub_len, dilation * width)
    z2 = z.reshape(sub_len, dilation * zw)
    v_blocks = zw // width
    v_block0 = v_col // width

    def cur(r, i):
        return (i, r)

    def prev(r, i):
        return (jnp.maximum(i * n_sub - 1, 0), r)

    def v_cur(r, i):
        return (i, r * v_blocks + v_block0)

    def v_prev(r, i):
        return (jnp.maximum(i * n_sub - 1, 0), r * v_blocks + v_block0)

    return pl.pallas_call(
        functools.partial(_attn_kernel, n_heads=n_heads, n_sub=n_sub),
        out_shape=(jax.ShapeDtypeStruct((sub_len, dilation * width), F32),
                   jax.ShapeDtypeStruct((sub_len, dilation * HEAD_DIM), F32)),
        grid=(dilation, sub_len // tq),
        in_specs=[pl.BlockSpec((tq, width), cur),
                  pl.BlockSpec((tq, width), cur),
                  pl.BlockSpec((ATTN_BLOCK, width), prev),
                  pl.BlockSpec((tq, width), v_cur),
                  pl.BlockSpec((ATTN_BLOCK, width), v_prev)],
        out_specs=(pl.BlockSpec((tq, width), cur),
                   pl.BlockSpec((tq, HEAD_DIM), cur)),
        compiler_params=_params(2),
        name=f"dilated_attn_d{dilation}",
    )(q2, k2, k2, z2, z2)


def _attn_merge_kernel(o1_ref, o2_ref, o3_ref, s1_ref, s2_ref, s3_ref, out_ref, *, n_heads):
    o_refs = (o1_ref, o2_ref, o3_ref)
    stats = (s1_ref[...], s2_ref[...], s3_ref[...])
    for h in range(n_heads):
        cs = slice(h * HEAD_DIM, (h + 1) * HEAD_DIM)
        ms = [st[:, 2 * h:2 * h + 1] for st in stats]
        ls = [st[:, 2 * h + 1:2 * h + 2] for st in stats]
        m_all = jnp.maximum(jnp.maximum(ms[0], ms[1]), ms[2])
        ws = [jnp.exp(m - m_all) for m in ms]
        den = ws[0] * ls[0] + ws[1] * ls[1] + ws[2] * ls[2]
        num = ws[0] * o_refs[0][:, cs] + ws[1] * o_refs[1][:, cs] + ws[2] * o_refs[2][:, cs]
        out_ref[:, cs] = (num / den).astype(out_ref.dtype)


def _attn_merge(parts, *, width, tm=256):
    s = parts[0][0].shape[0] * parts[0][0].shape[1] // width
    n_heads = width // HEAD_DIM
    outs = [o.reshape(s, width) for o, _ in parts]
    stats = [st.reshape(s, HEAD_DIM) for _, st in parts]
    o_spec = pl.BlockSpec((tm, width), lambda i: (i, 0))
    s_spec = pl.BlockSpec((tm, HEAD_DIM), lambda i: (i, 0))
    return pl.pallas_call(
        functools.partial(_attn_merge_kernel, n_heads=n_heads),
        out_shape=jax.ShapeDtypeStruct((s, width), BF16),
        grid=(s // tm,),
        in_specs=[o_spec] * 3 + [s_spec] * 3,
        out_specs=o_spec,
        compiler_params=_params(1),
        name="attn_merge",
    )(*outs, *stats)


def kernel(x, p, positions, norm_mix, w_in, v_ln_g, v_ln_b, w_spatial, b_spatial, q_norm, k_norm, w_br_a, w_br_b, w_out, norm_ffn, w_ffn_gate, w_ffn_up, w_ffn_down, norm_pl, w_pl_gate, w_pl_proj):
    batch, seq, d_model = x.shape
    assert batch == 1
    depth = w_in.shape[0]
    a_width = w_spatial.shape[1] * HEAD_DIM
    b_width = w_br_b.shape[1]
    d_ff = w_ffn_gate.shape[2]
    assert a_width == b_width
    u_col, va_col = 0, a_width
    q_col, k_col, v_col = 2 * a_width, 2 * a_width + b_width, 2 * a_width + 2 * b_width
    ga_col = 2 * a_width + 3 * b_width
    gb_col = ga_col + d_model
    assert w_in.shape[2] == gb_col + d_model

    tm = min(2048, seq)
    tn = 256
    tn_in = 512
    half_ff = d_ff // 2

    tables = _rope_tables(positions.reshape(seq))
    h = x.reshape(seq, d_model)
    for i in range(depth):
        xn = _rmsnorm(h, norm_mix[i])
        z = _matmul(_mm_plain_kernel, xn, w_in[i], [], [], out_dtype=BF16, tm=tm, tn=tn_in, name="in_proj")
        a_out = _gmlp(z, v_ln_g[i], v_ln_b[i], w_spatial[i], b_spatial[i], width=a_width)
        qn, kn = _qk_prep(z, q_norm[i], k_norm[i], tables, width=b_width, q_col=q_col, k_col=k_col)
        parts = [_dilated_attention_parts(qn, kn, z, d, width=b_width, v_col=v_col) for d in DILATIONS]
        b_out = _attn_merge(parts, width=b_width)
        t_a = _matmul(_mm_gate_kernel, a_out, w_br_a[i], [z], [_tile_spec(tm, tn, ga_col // tn)],
                      out_dtype=F32, tm=tm, tn=tn, name="branch_a")
        merged = _matmul(_mm_gate_add_kernel, b_out, w_br_b[i], [z, t_a],
                         [_tile_spec(tm, tn, gb_col // tn), _tile_spec(tm, tn)],
                         out_dtype=BF16, tm=tm, tn=tn, name="branch_b")
        h = _matmul(_mm_add_kernel, merged, w_out[i], [h], [_tile_spec(tm, tn)],
                    out_dtype=F32, tm=tm, tn=tn, name="out_proj")
        hn = _rmsnorm(h, norm_ffn[i])
        t = _ffn_up(hn, w_ffn_gate[i], w_ffn_up[i], tm=tm, tn=tn)
        for kb in range(2):
            h = _matmul(_mm_add_kernel, t, w_ffn_down[i], [h], [_tile_spec(tm // 2, tn)],
                        out_dtype=F32, tm=tm // 2, tn=tn, kb=kb, kblk=half_ff, name=f"ffn_down{kb}")
        hp = _rmsnorm(h, norm_pl[i])
        h = _matmul(_mm_plgate_kernel, hp, w_pl_gate[i], [p[i].reshape(seq, -1), w_pl_proj[i], h],
                    [pl.BlockSpec((tm, p.shape[-1]), lambda i_, j_: (i_, 0)),
                     pl.BlockSpec((p.shape[-1], tn), lambda i_, j_: (0, j_)),
                     _tile_spec(tm, tn)],
                    out_dtype=F32, tm=tm, tn=tn, name="pl_embed")
    return h.reshape(batch, seq, d_model)
```
